```python
import jax, jax.numpy as jnp
from jax import lax
import numpy as np

D_MODEL = 2048
BATCH = 2
SEQ = 16384
DEPTH = 4

N_META = 16
BLOCK = 128
PAD = BLOCK - N_META
WINDOW = 128
HEAD_DIM = 64
N_Q_HEADS = D_MODEL // HEAD_DIM
N_KV_HEADS = N_Q_HEADS // 8
ATTN_WIDTH = N_Q_HEADS * HEAD_DIM
KV_WIDTH = N_KV_HEADS * HEAD_DIM
ROPE_THETA = 10000.0
SSM_HEAD_DIM = 64
SSM_D_INNER = D_MODEL
N_SSM_HEADS = SSM_D_INNER // SSM_HEAD_DIM
N_SSM_GROUPS = 4
D_STATE = 128
CONV_K = 4
CONV_DIM = SSM_D_INNER + 2 * N_SSM_GROUPS * D_STATE
MIX_WIDTH = ATTN_WIDTH + SSM_D_INNER
IN_DIM = ATTN_WIDTH + 2 * KV_WIDTH + SSM_D_INNER + CONV_DIM + N_SSM_HEADS
SPLITS = (ATTN_WIDTH,
          ATTN_WIDTH + KV_WIDTH,
          ATTN_WIDTH + 2 * KV_WIDTH,
          ATTN_WIDTH + 2 * KV_WIDTH + SSM_D_INNER,
          ATTN_WIDTH + 2 * KV_WIDTH + SSM_D_INNER + CONV_DIM)
D_FF = ((8 * D_MODEL // 3 + 255) // 256) * 256
EPS = 1e-6

kernel_name = "hymba_swa_sink_mamba2_hybrid"


def rmsnorm(x, g):
    xf = x.astype(jnp.float32)
    y = xf * lax.rsqrt(jnp.mean(xf * xf, axis=-1, keepdims=True) + EPS) * g.astype(jnp.float32)
    return y.astype(x.dtype)


def rope(x, cos, sin):
    xf = x.astype(jnp.float32)
    x1, x2 = jnp.split(xf, 2, axis=-1)
    c = cos[None, :, None, :]
    s = sin[None, :, None, :]
    return jnp.concatenate([x1 * c - x2 * s, x2 * c + x1 * s], axis=-1).astype(x.dtype)


def banded_sink_attention(q, k, v, sinks, mask):
    b, L = q.shape[0], q.shape[1]
    nb = L // BLOCK
    rep = N_Q_HEADS // N_KV_HEADS
    q = q.reshape(b, nb, BLOCK, N_KV_HEADS, rep, HEAD_DIM)
    k = k.reshape(b, nb, BLOCK, N_KV_HEADS, HEAD_DIM)
    v = v.reshape(b, nb, BLOCK, N_KV_HEADS, HEAD_DIM)

    def with_prev(t):
        prev = jnp.pad(t, ((0, 0), (1, 0), (0, 0), (0, 0), (0, 0)))[:, :-1]
        return jnp.concatenate([prev, t], axis=2)

    k2, v2 = with_prev(k), with_prev(v)
    s = jnp.einsum('bnqgrd,bnkgd->bngrqk', q, k2,
                   preferred_element_type=jnp.float32) * (HEAD_DIM ** -0.5)
    s = jnp.where(mask[None, :, None, None], s, -jnp.inf)
    sink = sinks.astype(jnp.float32).reshape(1, 1, N_KV_HEADS, rep, 1, 1)
    m = jnp.maximum(jnp.max(s, axis=-1, keepdims=True), sink)
    e = jnp.exp(s - m)
    p = e / (jnp.sum(e, axis=-1, keepdims=True) + jnp.exp(sink - m))
    o = jnp.einsum('bngrqk,bnkgd->bnqgrd', p.astype(v.dtype), v2)
    return o.reshape(b, L, ATTN_WIDTH)


def causal_depthwise_conv(u, w, bias):
    out = lax.conv_general_dilated(u, w[:, None, :], window_strides=(1,),
                                   padding=[(CONV_K - 1, 0)],
                                   dimension_numbers=('NWC', 'WIO', 'NWC'),
                                   feature_group_count=u.shape[-1])
    return out + bias


def ssd_chunked(xh, dt, a, bm, cm):
    bsz, L = xh.shape[0], xh.shape[1]
    nc = L // BLOCK
    rep = N_SSM_HEADS // N_SSM_GROUPS
    X = (xh * dt[..., None]).reshape(bsz, nc, BLOCK, N_SSM_GROUPS, rep, SSM_HEAD_DIM)
    A = (dt * a).reshape(bsz, nc, BLOCK, N_SSM_GROUPS, rep).transpose(0, 1, 3, 4, 2)
    Bc = bm.reshape(bsz, nc, BLOCK, N_SSM_GROUPS, D_STATE)
    Cc = cm.reshape(bsz, nc, BLOCK, N_SSM_GROUPS, D_STATE)
    cs = jnp.cumsum(A, axis=-1)
    causal = jnp.tril(jnp.ones((BLOCK, BLOCK), dtype=bool))
    decay_in = jnp.exp(jnp.where(causal, cs[..., :, None] - cs[..., None, :], -jnp.inf))
    cb = jnp.einsum('bclgn,bcsgn->bcgls', Cc, Bc)
    y_diag = jnp.einsum('bcgrls,bcsgrp->bclgrp', cb[:, :, :, None] * decay_in, X)
    decay_states = jnp.exp(cs[..., -1:] - cs)
    states = jnp.einsum('bclgn,bcgrl,bclgrp->bcgrpn', Bc, decay_states, X)
    chunk_decay = jnp.exp(cs[..., -1])

    def step(carry, inp):
        dec, st = inp
        return carry * dec[..., None, None] + st, carry

    init = jnp.zeros_like(states[:, 0])
    _, prev = lax.scan(step, init, (jnp.moveaxis(chunk_decay, 1, 0), jnp.moveaxis(states, 1, 0)))
    prev = jnp.moveaxis(prev, 0, 1)
    y_off = jnp.einsum('bclgn,bcgrpn,bcgrl->bclgrp', Cc, prev, jnp.exp(cs))
    return (y_diag + y_off).reshape(bsz, L, N_SSM_HEADS, SSM_HEAD_DIM)


def mamba2_mixer(z, xbc, dt_raw, conv_w, conv_b, dt_bias, a_log, d_skip, norm_w, valid):
    bsz, L = z.shape[0], z.shape[1]
    xbc = jnp.where(valid[None, :, None], xbc, 0)
    xbc = jax.nn.silu(causal_depthwise_conv(xbc, conv_w, conv_b))
    xs, bm, cm = jnp.split(xbc, (SSM_D_INNER, SSM_D_INNER + N_SSM_GROUPS * D_STATE), axis=-1)
    xs = jnp.where(valid[None, :, None], xs, 0)
    dt = jax.nn.softplus(dt_raw.astype(jnp.float32) + dt_bias.astype(jnp.float32))
    a = -jnp.exp(a_log.astype(jnp.float32))
    xh = xs.astype(jnp.float32).reshape(bsz, L, N_SSM_HEADS, SSM_HEAD_DIM)
    y = ssd_chunked(xh, dt, a,
                    bm.astype(jnp.float32).reshape(bsz, L, N_SSM_GROUPS, D_STATE),
                    cm.astype(jnp.float32).reshape(bsz, L, N_SSM_GROUPS, D_STATE))
    y = y + d_skip.astype(jnp.float32)[:, None] * xh
    y = y.reshape(bsz, L, SSM_D_INNER) * jax.nn.silu(z.astype(jnp.float32))
    yg = y.reshape(bsz, L, N_SSM_GROUPS, SSM_D_INNER // N_SSM_GROUPS)
    yg = yg * lax.rsqrt(jnp.mean(yg * yg, axis=-1, keepdims=True) + EPS)
    y = yg.reshape(bsz, L, SSM_D_INNER) * norm_w.astype(jnp.float32)
    return y.astype(z.dtype)


def setup_inputs(seed: int = 0) -> dict:
    key = jax.random.key(seed)
    ks = jax.random.split(key, 20)
    f32 = jnp.float32
    nrm = lambda k, shape, scale: jax.random.normal(k, shape, f32) * scale
    dt0 = jnp.exp(jax.random.uniform(ks[9], (DEPTH, N_SSM_HEADS), f32,
                                     np.log(1e-3).astype(np.float32), np.log(1e-1).astype(np.float32)))
    return {
        "x": nrm(ks[0], (BATCH, SEQ, D_MODEL), 1.0),
        "meta_tokens": nrm(ks[1], (N_META, D_MODEL), 1.0),
        "norm_mix": 1.0 + nrm(ks[2], (DEPTH, D_MODEL), 0.02),
        "w_in": nrm(ks[3], (DEPTH, D_MODEL, IN_DIM), D_MODEL ** -0.5),
        "q_norm": 1.0 + nrm(ks[4], (DEPTH, HEAD_DIM), 0.02),
        "k_norm": 1.0 + nrm(ks[5], (DEPTH, HEAD_DIM), 0.02),
        "attn_sinks": nrm(ks[6], (DEPTH, N_Q_HEADS), 1.0),
        "conv_w": nrm(ks[7], (DEPTH, CONV_K, CONV_DIM), CONV_K ** -0.5),
        "conv_b": nrm(ks[8], (DEPTH, CONV_DIM), 0.02),
        "dt_bias": dt0 + jnp.log(-jnp.expm1(-dt0)),
        "a_log": jnp.log(jax.random.uniform(ks[10], (DEPTH, N_SSM_HEADS), f32, 1.0, 16.0)),
        "d_skip": 1.0 + nrm(ks[11], (DEPTH, N_SSM_HEADS), 0.02),
        "ssm_norm": 1.0 + nrm(ks[12], (DEPTH, SSM_D_INNER), 0.02),
        "w_out": nrm(ks[13], (DEPTH, MIX_WIDTH, D_MODEL), MIX_WIDTH ** -0.5),
        "norm_ffn": 1.0 + nrm(ks[14], (DEPTH, D_MODEL), 0.02),
        "w_gate": nrm(ks[15], (DEPTH, D_MODEL, D_FF), D_MODEL ** -0.5),
        "w_up": nrm(ks[16], (DEPTH, D_MODEL, D_FF), D_MODEL ** -0.5),
        "w_down": nrm(ks[17], (DEPTH, D_FF, D_MODEL), D_FF ** -0.5),
    }


def reference(x, meta_tokens, norm_mix, w_in, q_norm, k_norm, attn_sinks, conv_w, conv_b,
              dt_bias, a_log, d_skip, ssm_norm, w_out, norm_ffn, w_gate, w_up, w_down):
    bsz = x.shape[0]
    h = jnp.concatenate([jnp.zeros((bsz, PAD, D_MODEL), x.dtype),
                         jnp.broadcast_to(meta_tokens.astype(x.dtype)[None], (bsz, N_META, D_MODEL)),
                         x], axis=1)
    Lp = h.shape[1]
    nb = Lp // BLOCK
    idx = jnp.arange(Lp)
    valid = idx >= PAD
    pos = (idx - PAD).astype(jnp.float32)
    inv_freq = ROPE_THETA ** (-jnp.arange(0, HEAD_DIM, 2, dtype=jnp.float32) / HEAD_DIM)
    ang = pos[:, None] * inv_freq[None, :]
    cos, sin = jnp.cos(ang), jnp.sin(ang)
    qi = jnp.arange(nb)[:, None, None] * BLOCK + jnp.arange(BLOCK)[None, :, None]
    kj = jnp.arange(nb)[:, None, None] * BLOCK - BLOCK + jnp.arange(2 * BLOCK)[None, None, :]
    diff = qi - kj
    mask = (diff >= 0) & (diff < WINDOW) & (kj >= PAD)

    for i in range(DEPTH):
        u = rmsnorm(h, norm_mix[i])
        proj = u @ w_in[i]
        q, k, v, z, xbc, dt_raw = jnp.split(proj, SPLITS, axis=-1)
        q = rope(rmsnorm(q.reshape(bsz, Lp, N_Q_HEADS, HEAD_DIM), q_norm[i]), cos, sin)
        k = rope(rmsnorm(k.reshape(bsz, Lp, N_KV_HEADS, HEAD_DIM), k_norm[i]), cos, sin)
        v = v.reshape(bsz, Lp, N_KV_HEADS, HEAD_DIM)
        attn = banded_sink_attention(q, k, v, attn_sinks[i], mask)
        ssm = mamba2_mixer(z, xbc, dt_raw, conv_w[i], conv_b[i], dt_bias[i], a_log[i],
                           d_skip[i], ssm_norm[i], valid)
        h = h + jnp.concatenate([attn, ssm], axis=-1) @ w_out[i]
        u = rmsnorm(h, norm_ffn[i])
        h = h + (jax.nn.silu(u @ w_gate[i]) * (u @ w_up[i])) @ w_down[i]
    return h[:, BLOCK:]
```

```python
import functools

import numpy as np
import jax
import jax.numpy as jnp
from jax import lax
from jax.experimental import pallas as pl
from jax.experimental.pallas import tpu as pltpu

F32 = jnp.float32
BF16 = jnp.bfloat16

D_MODEL = 2048
DEPTH = 4
N_META = 16
BLOCK = 128
PAD = BLOCK - N_META
WINDOW = 128
HEAD_DIM = 64
N_Q_HEADS = D_MODEL // HEAD_DIM
N_KV_HEADS = N_Q_HEADS // 8
Q_PER_KV = N_Q_HEADS // N_KV_HEADS
ATTN_WIDTH = N_Q_HEADS * HEAD_DIM
KV_WIDTH = N_KV_HEADS * HEAD_DIM
ROPE_THETA = 10000.0
SSM_HEAD_DIM = 64
SSM_D_INNER = D_MODEL
N_SSM_HEADS = SSM_D_INNER // SSM_HEAD_DIM
N_SSM_GROUPS = 4
SSM_HEADS_PER_GROUP = N_SSM_HEADS // N_SSM_GROUPS
SSM_GROUP_WIDTH = SSM_D_INNER // N_SSM_GROUPS
D_STATE = 128
CONV_K = 4
BC_WIDTH = N_SSM_GROUPS * D_STATE
CONV_DIM = SSM_D_INNER + 2 * BC_WIDTH
MIX_WIDTH = ATTN_WIDTH + SSM_D_INNER
D_FF = ((8 * D_MODEL // 3 + 255) // 256) * 256
EPS = 1e-6
MAIN_WIDTH = ATTN_WIDTH + 2 * KV_WIDTH + SSM_D_INNER + CONV_DIM

LANES = 128
SUBLANES = 8
VMEM_LIMIT_BYTES = 56 * 1024 * 1024

IN_TN = 512
Q_TILES = ATTN_WIDTH // IN_TN
KV_TILE = Q_TILES
Z_TILE0 = KV_TILE + (2 * KV_WIDTH) // IN_TN
XBC_TILE0 = Z_TILE0 + SSM_D_INNER // IN_TN
IN_TILES = MAIN_WIDTH // IN_TN
FF_TN = 512
OUT_TN = 1024
NEG_BIG = -1e30


def _row_tile(m):
    for t in (768, 512, 384, 256, 128):
        if m % t == 0:
            return t
    raise ValueError(f"row count {m} must be a multiple of {BLOCK}")


def _sigmoid(x):
    return 1.0 / (1.0 + jnp.exp(-x))


def _split3(x):
    hi = x.astype(BF16)
    r1 = x - hi.astype(F32)
    mid = r1.astype(BF16)
    lo = (r1 - mid.astype(F32)).astype(BF16)
    return hi, mid, lo


def _headnorm_rope(x, gain, cos, sin_signed):
    rows = lax.broadcasted_iota(jnp.int32, (LANES, LANES), 0) // HEAD_DIM
    cols = lax.broadcasted_iota(jnp.int32, (LANES, LANES), 1) // HEAD_DIM
    seg = jnp.where(rows == cols, 1.0, 0.0).astype(BF16)
    hi, mid, _ = _split3(x * x)
    ss = (jnp.dot(hi, seg, preferred_element_type=F32)
          + jnp.dot(mid, seg, preferred_element_type=F32))
    xn = x * lax.rsqrt(ss * (1.0 / HEAD_DIM) + EPS) * gain
    lane = lax.broadcasted_iota(jnp.int32, xn.shape, 1)
    first_half = (lane % HEAD_DIM) < (HEAD_DIM // 2)
    partner = jnp.where(first_half,
                        pltpu.roll(xn, LANES - HEAD_DIM // 2, 1),
                        pltpu.roll(xn, HEAD_DIM // 2, 1))
    return xn * cos + partner * sin_signed


def _in_proj_kernel(h_ref, g_ref, w_ref, wdt_ref, qg_ref, kg_ref, cos_ref, sin_ref,
                    q_ref, kv_ref, z_ref, xbc_ref, dt_ref, u_scr):
    j = pl.program_id(1)

    @pl.when(j == 0)
    def _():
        x = h_ref[...]
        ms = jnp.mean(x * x, axis=-1, keepdims=True)
        u = (x * lax.rsqrt(ms + EPS) * g_ref[...]).astype(BF16)
        u_scr[...] = u
        dt_ref[...] = jnp.dot(u, wdt_ref[...], preferred_element_type=F32)

    acc = jnp.dot(u_scr[...], w_ref[...], preferred_element_type=F32)

    @pl.when(j < Q_TILES)
    def _():
        cos = cos_ref[...]
        sin = sin_ref[...]
        gain = qg_ref[...] * (HEAD_DIM ** -0.5)
        for c in range(IN_TN // LANES):
            sl = slice(c * LANES, (c + 1) * LANES)
            q_ref[:, sl] = _headnorm_rope(acc[:, sl], gain, cos, sin).astype(BF16)

    @pl.when(j == KV_TILE)
    def _():
        cos = cos_ref[...]
        sin = sin_ref[...]
        for c in range(KV_WIDTH // LANES):
            sl = slice(c * LANES, (c + 1) * LANES)
            kv_ref[:, sl] = _headnorm_rope(acc[:, sl], kg_ref[...], cos, sin).astype(BF16)
        kv_ref[:, KV_WIDTH:] = acc[:, KV_WIDTH:].astype(BF16)

    @pl.when((j >= Z_TILE0) & (j < XBC_TILE0))
    def _():
        z_ref[...] = acc

    @pl.when(j >= XBC_TILE0)
    def _():
        xbc_ref[...] = acc


def _in_proj(h, layer, norm_g, w_in_b, w_dt_b, q_gain, k_gain, cos_t, sin_t):
    m = h.shape[0]
    tm = _row_tile(m)
    clampj = lambda j, lo, n: jnp.clip(j - lo, 0, n - 1)
    return pl.pallas_call(
        _in_proj_kernel,
        grid=(m // tm, IN_TILES),
        in_specs=[
            pl.BlockSpec((tm, D_MODEL), lambda i, j: (i, 0)),
            pl.BlockSpec((None, 1, D_MODEL), lambda i, j: (layer, 0, 0)),
            pl.BlockSpec((None, D_MODEL, IN_TN), lambda i, j: (layer, 0, j)),
            pl.BlockSpec((None, D_MODEL, LANES), lambda i, j: (layer, 0, 0)),
            pl.BlockSpec((None, 1, LANES), lambda i, j: (layer, 0, 0)),
            pl.BlockSpec((None, 1, LANES), lambda i, j: (layer, 0, 0)),
            pl.BlockSpec((tm, LANES), lambda i, j: (i, 0)),
            pl.BlockSpec((tm, LANES), lambda i, j: (i, 0)),
        ],
        out_specs=[
            pl.BlockSpec((tm, IN_TN), lambda i, j: (i, clampj(j, 0, Q_TILES))),
            pl.BlockSpec((tm, 2 * KV_WIDTH), lambda i, j: (i, 0)),
            pl.BlockSpec((tm, IN_TN), lambda i, j: (i, clampj(j, Z_TILE0, SSM_D_INNER // IN_TN))),
            pl.BlockSpec((tm, IN_TN), lambda i, j: (i, clampj(j, XBC_TILE0, CONV_DIM // IN_TN))),
            pl.BlockSpec((tm, LANES), lambda i, j: (i, 0)),
        ],
        out_shape=[
            jax.ShapeDtypeStruct((m, ATTN_WIDTH), BF16),
            jax.ShapeDtypeStruct((m, 2 * KV_WIDTH), BF16),
            jax.ShapeDtypeStruct((m, SSM_D_INNER), F32),
            jax.ShapeDtypeStruct((m, CONV_DIM), F32),
            jax.ShapeDtypeStruct((m, LANES), F32),
        ],
        scratch_shapes=[pltpu.VMEM((tm, D_MODEL), BF16)],
        compiler_params=pltpu.CompilerParams(
            dimension_semantics=("arbitrary", "arbitrary"), vmem_limit_bytes=VMEM_LIMIT_BYTES),
        name="in_proj",
    )(h, norm_g, w_in_b, w_dt_b, q_gain, k_gain, cos_t, sin_t)


def _attn_kernel(sink_ref, q_ref, kvp_ref, kvc_ref, o_ref):
    n = pl.program_id(1)
    q = q_ref[...]
    kvp = kvp_ref[...]
    kvc = kvc_ref[...]
    qi = lax.broadcasted_iota(jnp.int32, (BLOCK, 2 * BLOCK), 0) + BLOCK
    kj = lax.broadcasted_iota(jnp.int32, (BLOCK, 2 * BLOCK), 1)
    diff = qi - kj
    mask = (diff >= 0) & (diff < WINDOW) & (kj + (n - 1) * BLOCK >= PAD)
    for g in range(N_KV_HEADS):
        ks = slice(g * HEAD_DIM, (g + 1) * HEAD_DIM)
        vs = slice(KV_WIDTH + g * HEAD_DIM, KV_WIDTH + (g + 1) * HEAD_DIM)
        k2 = jnp.concatenate([kvp[:, ks], kvc[:, ks]], axis=0)
        v2 = jnp.concatenate([kvp[:, vs], kvc[:, vs]], axis=0)
        qg = jnp.concatenate(
            [q[:, (g * Q_PER_KV + r) * HEAD_DIM:(g * Q_PER_KV + r + 1) * HEAD_DIM] for r in range(Q_PER_KV)],
            axis=0)
        s = lax.dot_general(qg, k2, (((1,), (1,)), ((), ())), preferred_element_type=F32)
        es, inv = [], []
        for r in range(Q_PER_KV):
            sink = sink_ref[g * Q_PER_KV + r]
            sr = jnp.where(mask, s[r * BLOCK:(r + 1) * BLOCK], NEG_BIG)
            mx = jnp.maximum(jnp.max(sr, axis=-1, keepdims=True), sink)
            e = jnp.exp(sr - mx)
            denom = jnp.sum(e, axis=-1, keepdims=True) + jnp.exp(sink - mx)
            es.append(e.astype(BF16))
            inv.append(1.0 / denom)
        o = jnp.dot(jnp.concatenate(es, axis=0), v2, preferred_element_type=F32)
        for r in range(Q_PER_KV):
            hq = g * Q_PER_KV + r
            o_ref[:, hq * HEAD_DIM:(hq + 1) * HEAD_DIM] = (o[r * BLOCK:(r + 1) * BLOCK] * inv[r]).astype(BF16)


def _attention(q, kv, sinks):
    b, lp, _ = q.shape
    nb = lp // BLOCK
    return pl.pallas_call(
        _attn_kernel,
        grid=(b, nb),
        in_specs=[
            pl.BlockSpec(memory_space=pltpu.SMEM),
            pl.BlockSpec((None, BLOCK, ATTN_WIDTH), lambda bi, n: (bi, n, 0)),
            pl.BlockSpec((None, BLOCK, 2 * KV_WIDTH), lambda bi, n: (bi, jnp.maximum(n - 1, 0), 0)),
            pl.BlockSpec((None, BLOCK, 2 * KV_WIDTH), lambda bi, n: (bi, n, 0)),
        ],
        out_specs=pl.BlockSpec((None, BLOCK, ATTN_WIDTH), lambda bi, n: (bi, n, 0)),
        out_shape=jax.ShapeDtypeStruct((b, lp, ATTN_WIDTH), BF16),
        compiler_params=pltpu.CompilerParams(
            dimension_semantics=("arbitrary", "arbitrary"), vmem_limit_bytes=VMEM_LIMIT_BYTES),
        name="attention",
    )(sinks, q, kv, kv)


def _ssd_kernel(xbc_ref, z_ref, dt_ref, convw_ref, convb_ref, dtb_ref, alog_ref, dskip_ref, nw_ref,
                tril_ref, o_ref, ext_scr, state_scr):
    c = pl.program_id(1)

    @pl.when(c == 0)
    def _():
        ext_scr[0:SUBLANES, :] = jnp.zeros((SUBLANES, CONV_DIM), F32)
        state_scr[...] = jnp.zeros(state_scr.shape, F32)

    row = lax.broadcasted_iota(jnp.int32, (BLOCK, 1), 0)
    valid = (row + c * BLOCK) >= PAD

    ext_scr[SUBLANES:SUBLANES + BLOCK, :] = jnp.where(valid, xbc_ref[...], 0.0)
    conv = convb_ref[...]
    for k in range(CONV_K):
        off = SUBLANES - (CONV_K - 1) + k
        conv = conv + ext_scr[off:off + BLOCK, :] * convw_ref[k:k + 1, :]
    ext_scr[0:SUBLANES, :] = ext_scr[BLOCK:BLOCK + SUBLANES, :]
    act = conv * _sigmoid(conv)

    dtr = dt_ref[...] + dtb_ref[...]
    dt = jnp.maximum(dtr, 0.0) + jnp.log1p(jnp.exp(-jnp.abs(dtr)))
    a = -jnp.exp(alog_ref[...])
    tril = tril_ref[...]
    a_hi, a_mid, a_lo = _split3(dt * a)
    cs = (jnp.dot(tril, a_hi, preferred_element_type=F32)
          + jnp.dot(tril, a_mid, preferred_element_type=F32)
          + jnp.dot(tril, a_lo, preferred_element_type=F32))
    cs_last = cs[BLOCK - 1:BLOCK, :]
    ecs = jnp.exp(cs)
    wst = dt * jnp.exp(cs_last - cs)
    cs_t = cs.T
    dt_t = dt.T

    li = lax.broadcasted_iota(jnp.int32, (BLOCK, BLOCK), 0)
    si = lax.broadcasted_iota(jnp.int32, (BLOCK, BLOCK), 1)
    causal = li >= si
    low_half = lax.broadcasted_iota(jnp.int32, (BLOCK, LANES), 1) < SSM_HEAD_DIM

    for g in range(N_SSM_GROUPS):
        gs = slice(g * SSM_GROUP_WIDTH, (g + 1) * SSM_GROUP_WIDTH)
        bsl = slice(SSM_D_INNER + g * D_STATE, SSM_D_INNER + (g + 1) * D_STATE)
        csl = slice(SSM_D_INNER + BC_WIDTH + g * D_STATE, SSM_D_INNER + BC_WIDTH + (g + 1) * D_STATE)
        bg = act[:, bsl].astype(BF16)
        cg = act[:, csl].astype(BF16)
        cb = lax.dot_general(cg, bg, (((1,), (1,)), ((), ())), preferred_element_type=F32)
        st_prev = state_scr[g]
        y_off = jnp.dot(cg, st_prev.astype(BF16), preferred_element_type=F32)

        yd_parts, ecs_parts, xd_parts, xs_parts = [], [], [], []
        for pr in range(SSM_HEADS_PER_GROUP // 2):
            h0 = g * SSM_HEADS_PER_GROUP + 2 * pr
            psl = slice(h0 * SSM_HEAD_DIM, (h0 + 2) * SSM_HEAD_DIM)
            xs_pair = jnp.where(valid, act[:, psl], 0.0)
            yd = None
            for t in range(2):
                h = h0 + t
                dec = jnp.where(causal, cs[:, h:h + 1] - cs_t[h:h + 1, :], NEG_BIG)
                lmat = (jnp.exp(dec) * cb * dt_t[h:h + 1, :]).astype(BF16)
                keep = low_half if t == 0 else jnp.logical_not(low_half)
                xs_h = jnp.where(keep, xs_pair, 0.0).astype(BF16)
                part = jnp.dot(lmat, xs_h, preferred_element_type=F32)
                yd = part if yd is None else yd + part
            yd_parts.append(yd)
            ecs_parts.append(jnp.where(low_half, ecs[:, h0:h0 + 1], ecs[:, h0 + 1:h0 + 2]))
            w_pair = jnp.where(low_half, wst[:, h0:h0 + 1], wst[:, h0 + 1:h0 + 2])
            xd_parts.append((xs_pair * w_pair).astype(BF16))
            xs_parts.append(xs_pair)
        y_diag = jnp.concatenate(yd_parts, axis=1)
        ecs_x = jnp.concatenate(ecs_parts, axis=1)
        xd = jnp.concatenate(xd_parts, axis=1)
        xs_g = jnp.concatenate(xs_parts, axis=1)

        st_new = lax.dot_general(bg, xd, (((0,), (0,)), ((), ())), preferred_element_type=F32)
        state_scr[g] = st_prev * ecs_x[BLOCK - 1:BLOCK, :] + st_new

        y = y_diag + y_off * ecs_x + dskip_ref[:, gs] * xs_g
        zg = z_ref[:, gs]
        y = y * (zg * _sigmoid(zg))
        ms = jnp.mean(y * y, axis=-1, keepdims=True)
        o_ref[:, gs] = (y * lax.rsqrt(ms + EPS) * nw_ref[:, gs]).astype(BF16)


def _ssd(xbc, z, dt, layer, conv_w, conv_b, dt_bias, a_log, d_skip, norm_w, tril):
    b, lp, _ = xbc.shape
    nc = lp // BLOCK
    lmap = lambda bi, c: (layer, 0, 0)
    return pl.pallas_call(
        _ssd_kernel,
        grid=(b, nc),
        in_specs=[
            pl.BlockSpec((None, BLOCK, CONV_DIM), lambda bi, c: (bi, c, 0)),
            pl.BlockSpec((None, BLOCK, SSM_D_INNER), lambda bi, c: (bi, c, 0)),
            pl.BlockSpec((None, BLOCK, LANES), lambda bi, c: (bi, c, 0)),
            pl.BlockSpec((None, CONV_K, CONV_DIM), lmap),
            pl.BlockSpec((None, 1, CONV_DIM), lmap),
            pl.BlockSpec((None, 1, LANES), lmap),
            pl.BlockSpec((None, 1, LANES), lmap),
            pl.BlockSpec((None, 1, SSM_D_INNER), lmap),
            pl.BlockSpec((None, 1, SSM_D_INNER), lmap),
            pl.BlockSpec((BLOCK, BLOCK), lambda bi, c: (0, 0)),
        ],
        out_specs=pl.BlockSpec((None, BLOCK, SSM_D_INNER), lambda bi, c: (bi, c, 0)),
        out_shape=jax.ShapeDtypeStruct((b, lp, SSM_D_INNER), BF16),
        scratch_shapes=[
            pltpu.VMEM((SUBLANES + BLOCK, CONV_DIM), F32),
            pltpu.VMEM((N_SSM_GROUPS, D_STATE, SSM_GROUP_WIDTH), F32),
        ],
        compiler_params=pltpu.CompilerParams(
            dimension_semantics=("arbitrary", "arbitrary"), vmem_limit_bytes=VMEM_LIMIT_BYTES),
        name="ssd",
    )(xbc, z, dt, conv_w, conv_b, dt_bias, a_log, d_skip, norm_w, tril)


def _out_proj_kernel(h_ref, attn_ref, ssm_ref, wa_ref, ws_ref, o_ref):
    o_ref[...] = (h_ref[...]
                  + jnp.dot(attn_ref[...], wa_ref[...], preferred_element_type=F32)
                  + jnp.dot(ssm_ref[...], ws_ref[...], preferred_element_type=F32))


def _out_proj(h, attn, ssm, layer, w_out_b):
    m = h.shape[0]
    tm = _row_tile(m)
    return pl.pallas_call(
        _out_proj_kernel,
        grid=(D_MODEL // OUT_TN, m // tm),
        in_specs=[
            pl.BlockSpec((tm, OUT_TN), lambda j, i: (i, j)),
            pl.BlockSpec((tm, ATTN_WIDTH), lambda j, i: (i, 0)),
            pl.BlockSpec((tm, SSM_D_INNER), lambda j, i: (i, 0)),
            pl.BlockSpec((None, None, ATTN_WIDTH, OUT_TN), lambda j, i: (layer, 0, 0, j)),
            pl.BlockSpec((None, None, SSM_D_INNER, OUT_TN), lambda j, i: (layer, 1, 0, j)),
        ],
        out_specs=pl.BlockSpec((tm, OUT_TN), lambda j, i: (i, j)),
        out_shape=jax.ShapeDtypeStruct((m, D_MODEL), F32),
        compiler_params=pltpu.CompilerParams(
            dimension_semantics=("arbitrary", "arbitrary"), vmem_limit_bytes=VMEM_LIMIT_BYTES),
        name="out_proj",
    )(h, attn, ssm, w_out_b, w_out_b)


def _ffn_kernel(h_ref, g_ref, wg_ref, wu_ref, wd_ref, o_ref, u_scr):
    j = pl.program_id(1)

    @pl.when(j == 0)
    def _():
        x = h_ref[...]
        ms = jnp.mean(x * x, axis=-1, keepdims=True)
        u_scr[...] = (x * lax.rsqrt(ms + EPS) * g_ref[...]).astype(BF16)
        o_ref[...] = x

    u = u_scr[...]
    gate = jnp.dot(u, wg_ref[...], preferred_element_type=F32)
    up = jnp.dot(u, wu_ref[...], preferred_element_type=F32)
    act = (gate * _sigmoid(gate) * up).astype(BF16)
    o_ref[...] += jnp.dot(act, wd_ref[...], preferred_element_type=F32)


def _ffn(h, layer, norm_g, w_gate_b, w_up_b, w_down_b):
    m = h.shape[0]
    tm = _row_tile(m)
    return pl.pallas_call(
        _ffn_kernel,
        grid=(m // tm, D_FF // FF_TN),
        in_specs=[
            pl.BlockSpec((tm, D_MODEL), lambda i, j: (i, 0)),
            pl.BlockSpec((None, 1, D_MODEL), lambda i, j: (layer, 0, 0)),
            pl.BlockSpec((None, D_MODEL, FF_TN), lambda i, j: (layer, 0, j)),
            pl.BlockSpec((None, D_MODEL, FF_TN), lambda i, j: (layer, 0, j)),
            pl.BlockSpec((None, FF_TN, D_MODEL), lambda i, j: (layer, j, 0)),
        ],
        out_specs=pl.BlockSpec((tm, D_MODEL), lambda i, j: (i, 0)),
        out_shape=jax.ShapeDtypeStruct((m, D_MODEL), F32),
        scratch_shapes=[pltpu.VMEM((tm, D_MODEL), BF16)],
        compiler_params=pltpu.CompilerParams(
            dimension_semantics=("arbitrary", "arbitrary"), vmem_limit_bytes=VMEM_LIMIT_BYTES),
        name="ffn",
    )(h, norm_g, w_gate_b, w_up_b, w_down_b)


def _rope_tables(lp, batch):
    pos = (jnp.arange(lp) - PAD).astype(F32)
    inv_freq = ROPE_THETA ** (-jnp.arange(0, HEAD_DIM, 2, dtype=F32) / HEAD_DIM)
    ang = pos[:, None] * inv_freq[None, :]
    cos, sin = jnp.cos(ang), jnp.sin(ang)
    cos_t = jnp.tile(cos, (batch, 2 * LANES // HEAD_DIM))
    sin_t = jnp.tile(jnp.concatenate([-sin, sin], axis=-1), (batch, LANES // HEAD_DIM))
    return cos_t, sin_t


def _lane_pad(v):
    return jnp.pad(v.astype(F32), ((0, 0), (0, LANES - v.shape[-1])))[:, None, :]


def kernel(x, meta_tokens, norm_mix, w_in, q_norm, k_norm, attn_sinks, conv_w, conv_b, dt_bias, a_log,
           d_skip, ssm_norm, w_out, norm_ffn, w_gate, w_up, w_down):
    bsz, seq, _ = x.shape
    lp = seq + BLOCK
    m = bsz * lp
    h = jnp.concatenate([jnp.zeros((bsz, PAD, D_MODEL), x.dtype),
                         jnp.broadcast_to(meta_tokens.astype(x.dtype)[None], (bsz, N_META, D_MODEL)),
                         x], axis=1).reshape(m, D_MODEL)
    cos_t, sin_t = _rope_tables(lp, bsz)
    tril = jnp.asarray(np.tril(np.ones((BLOCK, BLOCK), np.float32)), BF16)

    w_in_b = w_in.astype(BF16)
    w_dt_b = jnp.pad(w_in_b[:, :, MAIN_WIDTH:], ((0, 0), (0, 0), (0, LANES - N_SSM_HEADS)))
    w_out_b = w_out.astype(BF16).reshape(DEPTH, 2, ATTN_WIDTH, D_MODEL)
    w_gate_b, w_up_b, w_down_b = w_gate.astype(BF16), w_up.astype(BF16), w_down.astype(BF16)
    norm_mix3 = norm_mix.astype(F32)[:, None, :]
    norm_ffn3 = norm_ffn.astype(F32)[:, None, :]
    q_gain = jnp.tile(q_norm.astype(F32), (1, LANES // HEAD_DIM))[:, None, :]
    k_gain = jnp.tile(k_norm.astype(F32), (1, LANES // HEAD_DIM))[:, None, :]
    conv_b3 = conv_b.astype(F32)[:, None, :]
    dt_bias3, a_log3 = _lane_pad(dt_bias), _lane_pad(a_log)
    d_skip3 = jnp.repeat(d_skip.astype(F32), SSM_HEAD_DIM, axis=-1)[:, None, :]
    ssm_norm3 = ssm_norm.astype(F32)[:, None, :]
    sinks = attn_sinks.astype(F32)

    for layer in range(DEPTH):
        q, kv, z, xbc, dt = _in_proj(h, layer, norm_mix3, w_in_b, w_dt_b, q_gain, k_gain, cos_t, sin_t)
        attn = _attention(q.reshape(bsz, lp, ATTN_WIDTH), kv.reshape(bsz, lp, 2 * KV_WIDTH), sinks[layer])
        ssm = _ssd(xbc.reshape(bsz, lp, CONV_DIM), z.reshape(bsz, lp, SSM_D_INNER), dt.reshape(bsz, lp, LANES),
                   layer, conv_w.astype(F32), conv_b3, dt_bias3, a_log3, d_skip3, ssm_norm3, tril)
        h = _out_proj(h, attn.reshape(m, ATTN_WIDTH), ssm.reshape(m, SSM_D_INNER), layer, w_out_b)
        h = _ffn(h, layer, norm_ffn3, w_gate_b, w_up_b, w_down_b)
    return h.reshape(bsz, lp, D_MODEL)[:, BLOCK:]
```
